```python
import math
import jax
import jax.numpy as jnp
from jax import lax
import numpy as np

D_MODEL = 1024
BATCH = 8
SEQ = 4096
DEPTH = 2

GRID_W = 64
CTX_LEN = 256
N_MIXERS = 2
NORM_EPS = 1e-6

MLA_HEADS = 16
Q_LORA = 384
KV_LORA = 256
QK_NOPE = 64
QK_ROPE = 32
V_DIM = 64
ROPE_BASE = 10000.0
Q_BLOCK = 128

HY_ORDER = 2
HY_EMB = 33
HY_BANDS = (HY_EMB - 1) // 2
HY_FFN = 64
HY_SHORT = 3
HY_TARGET = 1e-2
HY_FAST_PCT = 0.3
HY_SLOW_PCT = 1.5

PEER_HEADS = 8
PEER_NK = 128
PEER_EXPERTS = PEER_NK * PEER_NK
PEER_DK = 256
PEER_TOPK = 16
PEER_CHUNK = 512

kernel_name = "hybrid_mla_hyena_peer_diffusion_block"


def rmsnorm(x, g):
    xf = x.astype(jnp.float32)
    y = xf * lax.rsqrt(jnp.mean(xf * xf, axis=-1, keepdims=True) + NORM_EPS)
    return y.astype(x.dtype) * g


def modulate(h, shift, scale):
    return h * (1 + scale) + shift


def axial_rope_tables(n_tokens, dtype):
    t = jnp.arange(n_tokens, dtype=jnp.int32)
    row = (t // GRID_W).astype(jnp.float32)
    col = (t % GRID_W).astype(jnp.float32)
    n_freq = QK_ROPE // 4
    inv = ROPE_BASE ** (-jnp.arange(n_freq, dtype=jnp.float32) / n_freq)
    ar = row[:, None] * inv
    ac = col[:, None] * inv
    ang = jnp.concatenate([ar, ar, ac, ac], axis=-1)[:, None, :]
    return jnp.cos(ang).astype(dtype), jnp.sin(ang).astype(dtype)


def _rotate_half(x):
    x1, x2 = jnp.split(x, 2, axis=-1)
    return jnp.concatenate([-x2, x1], axis=-1)


def apply_axial_rope(x, cos, sin):
    xr, xc = jnp.split(x, 2, axis=-1)
    rot = jnp.concatenate([_rotate_half(xr), _rotate_half(xc)], axis=-1)
    return x * cos + rot * sin


def mla_queries(lat, g_q, w_uq, rope):
    B, L, _ = lat.shape
    cq = rmsnorm(lat[..., :Q_LORA], g_q)
    q = (cq @ w_uq).reshape(B, L, MLA_HEADS, QK_NOPE + QK_ROPE)
    q_nope, q_rope = q[..., :QK_NOPE], q[..., QK_NOPE:]
    if rope is not None:
        q_rope = apply_axial_rope(q_rope, *rope)
    return jnp.concatenate([q_nope, q_rope], axis=-1)


def mla_keys_values(lat, g_kv, w_ukv, rope):
    B, L, _ = lat.shape
    ckv = rmsnorm(lat[..., Q_LORA:Q_LORA + KV_LORA], g_kv)
    kv = (ckv @ w_ukv).reshape(B, L, MLA_HEADS, QK_NOPE + V_DIM)
    k_rope = lat[..., Q_LORA + KV_LORA:][:, :, None, :]
    if rope is not None:
        k_rope = apply_axial_rope(k_rope, *rope)
    k = jnp.concatenate([kv[..., :QK_NOPE], jnp.broadcast_to(k_rope, (B, L, MLA_HEADS, QK_ROPE))], axis=-1)
    return k, kv[..., QK_NOPE:]


def block_attention(q, k, v):
    B, L, H, dq = q.shape
    nb = L // Q_BLOCK
    qb = q.reshape(B, nb, Q_BLOCK, H, dq).transpose(1, 0, 2, 3, 4)
    scale = dq ** -0.5

    def one_block(qi):
        s = jnp.einsum('bqhd,bkhd->bhqk', qi, k).astype(jnp.float32) * scale
        p = jax.nn.softmax(s, axis=-1).astype(v.dtype)
        return jnp.einsum('bhqk,bkhd->bqhd', p, v)

    o = lax.map(one_block, qb)
    return o.transpose(1, 0, 2, 3, 4).reshape(B, L, H * v.shape[-1])


def hyena_filters(L, w1, b1, w2, b2, w3, b3, w4, sin_freq, deltas):
    f32 = jnp.float32
    t = jnp.linspace(0.0, 1.0, L, dtype=f32)[:, None]
    w = (2.0 * math.pi / L) * jnp.arange(L, dtype=f32)[:, None]
    f = jnp.linspace(1e-4, HY_BANDS - 1, HY_BANDS, dtype=f32)[None, :]
    z = jnp.concatenate([t, jnp.cos(f * w), -jnp.sin(f * w)], axis=-1).astype(w1.dtype)
    a = jnp.sin(sin_freq * (z @ w1 + b1))
    a = jnp.sin(sin_freq * (a @ w2 + b2))
    a = jnp.sin(sin_freq * (a @ w3 + b3))
    d = w4.shape[-1] // (2 * HY_ORDER)
    hf = (a @ w4).astype(f32).reshape(L, HY_ORDER, 2, d)
    decay = jnp.exp(-t[:, :, None] * jnp.abs(deltas.astype(f32)))
    hf = hf * decay[:, :, None, :]
    fwd, bwd = hf[:, :, 0], hf[:, :, 1]
    k = jnp.concatenate([fwd, jnp.zeros((1, HY_ORDER, d), f32), bwd[:0:-1]], axis=0)
    k = k / jnp.sum(jnp.abs(k), axis=0, keepdims=True)
    return jnp.fft.rfft(k, axis=0)


def fft_long_conv(u, kf):
    L = u.shape[1]
    U = jnp.fft.rfft(u.astype(jnp.float32), n=2 * L, axis=1)
    y = jnp.fft.irfft(U * kf[None], n=2 * L, axis=1)[:, :L]
    return y.astype(u.dtype)


def hyena_mixer(h, w_in, conv_w, conv_b, w1, b1, w2, b2, w3, b3, w4, sin_freq, deltas, bias, w_out):
    L = h.shape[1]
    p = h @ w_in
    pad = HY_SHORT // 2
    pp = jnp.pad(p, ((0, 0), (pad, pad), (0, 0)))
    acc = conv_b
    for tap in range(HY_SHORT):
        acc = acc + conv_w[tap] * pp[:, tap:tap + L]
    parts = jnp.split(acc, HY_ORDER + 1, axis=-1)
    z = parts[0]
    kf = hyena_filters(L, w1, b1, w2, b2, w3, b3, w4, sin_freq, deltas)
    for n in range(HY_ORDER):
        z = parts[n + 1] * (fft_long_conv(z, kf[:, n]) + bias[n] * z)
    return z @ w_out


def peer(h, wq, keys, u, v):
    shp = h.shape
    D = shp[-1]
    xt = h.reshape(-1, D)
    T = xt.shape[0]
    chunk = math.gcd(T, PEER_CHUNK)

    def one_chunk(xc):
        q = (xc @ wq).reshape(chunk, PEER_HEADS, 2, PEER_DK // 2)
        s = jnp.einsum('thpk,hpnk->thpn', q, keys).astype(jnp.float32)
        sv, si = lax.top_k(s, PEER_TOPK)
        cand = (sv[:, :, 0, :, None] + sv[:, :, 1, None, :]).reshape(chunk, PEER_HEADS, PEER_TOPK * PEER_TOPK)
        cidx = (si[:, :, 0, :, None] * PEER_NK + si[:, :, 1, None, :]).reshape(chunk, PEER_HEADS, PEER_TOPK * PEER_TOPK)
        fs, fi = lax.top_k(cand, PEER_TOPK)
        eidx = jnp.take_along_axis(cidx, fi, axis=-1)
        g = jax.nn.softmax(fs, axis=-1).astype(xc.dtype)
        act = jax.nn.gelu(jnp.einsum('td,thkd->thk', xc, u[eidx]), approximate=False)
        return jnp.einsum('thk,thkd->td', g * act, v[eidx])

    out = lax.map(one_chunk, xt.reshape(T // chunk, chunk, D))
    return out.reshape(shp)


def setup_inputs(seed: int = 0) -> dict:
    key = jax.random.key(seed)
    ks = iter(jax.random.split(key, 48))

    def nrm(shape, scale):
        return jax.random.normal(next(ks), shape, jnp.float32) * scale

    def gain(shape):
        return 1.0 + nrm(shape, 0.02)

    D = D_MODEL
    n_att = (DEPTH + N_MIXERS - 1) // N_MIXERS
    n_hy = DEPTH // N_MIXERS
    decay_lo = -math.log(HY_TARGET) / HY_SLOW_PCT
    decay_hi = -math.log(HY_TARGET) / HY_FAST_PCT
    deltas = jnp.linspace(decay_lo, decay_hi, D, dtype=jnp.float32)
    return {
        "x": nrm((BATCH, SEQ, D), 1.0),
        "c": nrm((BATCH, D), 1.0),
        "ctx": nrm((BATCH, CTX_LEN, D), 1.0),
        "c_ctx": nrm((D,), 1.0),
        "mod_w": nrm((DEPTH, D, 6 * D), 0.5 * D ** -0.5),
        "mod_b": nrm((DEPTH, 6 * D), 0.02),
        "norm_mix_g": gain((DEPTH, D)),
        "norm_ffn_g": gain((DEPTH, D)),
        "final_g": gain((D,)),
        "mla_w_dkv": nrm((n_att, D, Q_LORA + KV_LORA + QK_ROPE), D ** -0.5),
        "mla_q_norm_g": gain((n_att, Q_LORA)),
        "mla_w_uq": nrm((n_att, Q_LORA, MLA_HEADS * (QK_NOPE + QK_ROPE)), Q_LORA ** -0.5),
        "mla_kv_norm_g": gain((n_att, KV_LORA)),
        "mla_w_ukv": nrm((n_att, KV_LORA, MLA_HEADS * (QK_NOPE + V_DIM)), KV_LORA ** -0.5),
        "mla_w_o": nrm((n_att, MLA_HEADS * V_DIM, D), (MLA_HEADS * V_DIM) ** -0.5),
        "hy_w_in": nrm((n_hy, D, (HY_ORDER + 1) * D), D ** -0.5),
        "hy_conv_w": nrm((n_hy, HY_SHORT, (HY_ORDER + 1) * D), HY_SHORT ** -0.5),
        "hy_conv_b": nrm((n_hy, (HY_ORDER + 1) * D), 0.02),
        "hy_w1": nrm((n_hy, HY_EMB, HY_FFN), HY_EMB ** -0.5),
        "hy_b1": nrm((n_hy, HY_FFN), 0.1),
        "hy_w2": nrm((n_hy, HY_FFN, HY_FFN), HY_FFN ** -0.5),
        "hy_b2": nrm((n_hy, HY_FFN), 0.1),
        "hy_w3": nrm((n_hy, HY_FFN, HY_FFN), HY_FFN ** -0.5),
        "hy_b3": nrm((n_hy, HY_FFN), 0.1),
        "hy_w4": nrm((n_hy, HY_FFN, HY_ORDER * 2 * D), HY_FFN ** -0.5),
        "hy_sin_freq": gain((n_hy, HY_FFN)),
        "hy_deltas": deltas * gain((n_hy, HY_ORDER, D)),
        "hy_bias": nrm((n_hy, HY_ORDER, D), 1.0),
        "hy_w_out": nrm((n_hy, D, D), D ** -0.5),
        "peer_wq": nrm((DEPTH, D, PEER_HEADS * PEER_DK), D ** -0.5),
        "peer_keys": nrm((DEPTH, PEER_HEADS, 2, PEER_NK, PEER_DK // 2), (PEER_DK // 2) ** -0.5),
        "peer_u": nrm((DEPTH, PEER_EXPERTS, D), D ** -0.5),
        "peer_v": nrm((DEPTH, PEER_EXPERTS, D), 1.0),
    }


def reference(x, c, ctx, c_ctx, mod_w, mod_b, norm_mix_g, norm_ffn_g, final_g,
              mla_w_dkv, mla_q_norm_g, mla_w_uq, mla_kv_norm_g, mla_w_ukv, mla_w_o,
              hy_w_in, hy_conv_w, hy_conv_b, hy_w1, hy_b1, hy_w2, hy_b2, hy_w3, hy_b3, hy_w4,
              hy_sin_freq, hy_deltas, hy_bias, hy_w_out,
              peer_wq, peer_keys, peer_u, peer_v):
    S = x.shape[1]
    rope = axial_rope_tables(S, x.dtype)
    sc = jax.nn.silu(c)
    sc_ctx = jax.nn.silu(c_ctx)
    for i in range(DEPTH):
        j = i // N_MIXERS
        ctx_read_later = any(l % N_MIXERS == 0 for l in range(i + 1, DEPTH))
        m_lat = jnp.split((sc @ mod_w[i] + mod_b[i])[:, None, :], 6, axis=-1)
        m_ctx = jnp.split(sc_ctx @ mod_w[i] + mod_b[i], 6, axis=-1)

        hl = modulate(rmsnorm(x, norm_mix_g[i]), m_lat[0], m_lat[1])
        if i % N_MIXERS == 0:
            hc = modulate(rmsnorm(ctx, norm_mix_g[i]), m_ctx[0], m_ctx[1])
            lat_l = hl @ mla_w_dkv[j]
            lat_c = hc @ mla_w_dkv[j]
            kl, vl = mla_keys_values(lat_l, mla_kv_norm_g[j], mla_w_ukv[j], rope)
            kc, vc = mla_keys_values(lat_c, mla_kv_norm_g[j], mla_w_ukv[j], None)
            ql = mla_queries(lat_l, mla_q_norm_g[j], mla_w_uq[j], rope)
            k_all = jnp.concatenate([kl, kc], axis=1)
            v_all = jnp.concatenate([vl, vc], axis=1)
            yl = block_attention(ql, k_all, v_all) @ mla_w_o[j]
            if ctx_read_later:
                qc = mla_queries(lat_c, mla_q_norm_g[j], mla_w_uq[j], None)
                yc = block_attention(qc, kc, vc) @ mla_w_o[j]
        else:
            hy = (hy_w_in[j], hy_conv_w[j], hy_conv_b[j], hy_w1[j], hy_b1[j], hy_w2[j], hy_b2[j],
                  hy_w3[j], hy_b3[j], hy_w4[j], hy_sin_freq[j], hy_deltas[j], hy_bias[j], hy_w_out[j])
            yl = hyena_mixer(hl, *hy)
            if ctx_read_later:
                hc = modulate(rmsnorm(ctx, norm_mix_g[i]), m_ctx[0], m_ctx[1])
                yc = hyena_mixer(hc, *hy)
        x = x + m_lat[2] * yl
        if ctx_read_later:
            ctx = ctx + m_ctx[2] * yc

        hl = modulate(rmsnorm(x, norm_ffn_g[i]), m_lat[3], m_lat[4])
        x = x + m_lat[5] * peer(hl, peer_wq[i], peer_keys[i], peer_u[i], peer_v[i])
        if ctx_read_later:
            hc = modulate(rmsnorm(ctx, norm_ffn_g[i]), m_ctx[3], m_ctx[4])
            ctx = ctx + m_ctx[5] * peer(hc, peer_wq[i], peer_keys[i], peer_u[i], peer_v[i])
    return rmsnorm(x, final_g)
```

```python
import functools
import math

import jax
import jax.numpy as jnp
from jax import lax
from jax.experimental import pallas as pl
from jax.experimental.pallas import tpu as pltpu

F32 = jnp.float32
BF16 = jnp.bfloat16

NORM_EPS = 1e-6
GRID_W = 64
ROPE_BASE = 10000.0

MLA_HEADS = 16
Q_LORA = 384
KV_LORA = 256
QK_NOPE = 64
QK_ROPE = 32
V_DIM = 64
LANES = 128

HY_ORDER = 2
HY_EMB = 33
HY_BANDS = (HY_EMB - 1) // 2
HY_SHORT = 3

PEER_HEADS = 8
PEER_NK = 128
PEER_TOPK = 16
PEER_PAIRS = PEER_HEADS * PEER_TOPK
PAIR_SHIFT = PEER_PAIRS.bit_length() - 1
assert 1 << PAIR_SHIFT == PEER_PAIRS

VMEM_LIMIT = 56 * 1024 * 1024


def _cparams(n_axes):
    return pltpu.CompilerParams(dimension_semantics=("arbitrary",) * n_axes, vmem_limit_bytes=VMEM_LIMIT)


def _rms(x, g):
    return x * lax.rsqrt(jnp.mean(x * x, axis=-1, keepdims=True) + NORM_EPS) * g


def _mm_bias_kernel(a_ref, w_ref, b_ref, o_ref):
    o_ref[...] = jnp.dot(a_ref[...], w_ref[...], preferred_element_type=F32,
                         precision=lax.Precision.HIGHEST) + b_ref[...]


def _mm_bias(a, w, b, tn):
    m, k = a.shape
    n = w.shape[1]
    return pl.pallas_call(
        _mm_bias_kernel,
        grid=(n // tn,),
        in_specs=[pl.BlockSpec((m, k), lambda j: (0, 0)),
                  pl.BlockSpec((k, tn), lambda j: (0, j)),
                  pl.BlockSpec((1, tn), lambda j: (0, j))],
        out_specs=pl.BlockSpec((m, tn), lambda j: (0, j)),
        out_shape=jax.ShapeDtypeStruct((m, n), F32),
        compiler_params=_cparams(1),
        name="mod_matmul",
    )(a, w, b.reshape(1, n))


def _mm_kernel(a_ref, b_ref, o_ref, acc_ref):
    @pl.when(pl.program_id(2) == 0)
    def _():
        acc_ref[...] = jnp.zeros_like(acc_ref)

    acc_ref[...] += jnp.dot(a_ref[...], b_ref[...], preferred_element_type=F32)

    @pl.when(pl.program_id(2) == pl.num_programs(2) - 1)
    def _():
        o_ref[...] = acc_ref[...].astype(o_ref.dtype)


def _mm(a, b, tm, tn, tk, out_dtype=F32, name="matmul"):
    m, k = a.shape
    n = b.shape[1]
    return pl.pallas_call(
        _mm_kernel,
        grid=(m // tm, n // tn, k // tk),
        in_specs=[pl.BlockSpec((tm, tk), lambda i, j, l: (i, l)),
                  pl.BlockSpec((tk, tn), lambda i, j, l: (l, j))],
        out_specs=pl.BlockSpec((tm, tn), lambda i, j, l: (i, j)),
        out_shape=jax.ShapeDtypeStruct((m, n), out_dtype),
        scratch_shapes=[pltpu.VMEM((tm, tn), F32)],
        compiler_params=_cparams(3),
        name=name,
    )(a, b)


def _norm_mod_mm_kernel(x_ref, g_ref, sh_ref, sc_ref, w_ref, y_ref, *h_ref):
    h = _rms(x_ref[...], g_ref[...]) * (1.0 + sc_ref[0]) + sh_ref[0]
    if h_ref:
        h_ref[0][...] = h
    y_ref[...] = jnp.dot(h.astype(BF16), w_ref[...], preferred_element_type=F32).astype(y_ref.dtype)


def _norm_mod_mm(x, g, shift, scale, w, tm, out_dtype=F32, with_h=False, name="norm_mod_matmul"):
    t, d = x.shape
    n = w.shape[1]
    nb = shift.shape[0]
    blocks_per_batch = t // nb // tm
    bidx = lambda i: (i // blocks_per_batch, 0, 0)
    out_shape = [jax.ShapeDtypeStruct((t, n), out_dtype)]
    out_specs = [pl.BlockSpec((tm, n), lambda i: (i, 0))]
    if with_h:
        out_shape.append(jax.ShapeDtypeStruct((t, d), F32))
        out_specs.append(pl.BlockSpec((tm, d), lambda i: (i, 0)))
    res = pl.pallas_call(
        _norm_mod_mm_kernel,
        grid=(t // tm,),
        in_specs=[pl.BlockSpec((tm, d), lambda i: (i, 0)),
                  pl.BlockSpec((1, d), lambda i: (0, 0)),
                  pl.BlockSpec((1, 1, d), bidx),
                  pl.BlockSpec((1, 1, d), bidx),
                  pl.BlockSpec((d, n), lambda i: (0, 0))],
        out_specs=out_specs,
        out_shape=out_shape,
        compiler_params=_cparams(1),
        name=name,
    )(x, g.reshape(1, d), shift, scale, w)
    return res if with_h else res[0]


def _resid_mm_kernel(a_ref, w_ref, x_ref, gt_ref, o_ref):
    y = jnp.dot(a_ref[...].astype(BF16), w_ref[...], preferred_element_type=F32)
    o_ref[...] = x_ref[...] + gt_ref[0] * y


def _resid_mm(a, w, x, gate, tm, name="resid_matmul"):
    t, k = a.shape
    d = w.shape[1]
    nb = gate.shape[0]
    blocks_per_batch = t // nb // tm
    return pl.pallas_call(
        _resid_mm_kernel,
        grid=(t // tm,),
        in_specs=[pl.BlockSpec((tm, k), lambda i: (i, 0)),
                  pl.BlockSpec((k, d), lambda i: (0, 0)),
                  pl.BlockSpec((tm, d), lambda i: (i, 0)),
                  pl.BlockSpec((1, 1, d), lambda i: (i // blocks_per_batch, 0, 0))],
        out_specs=pl.BlockSpec((tm, d), lambda i: (i, 0)),
        out_shape=jax.ShapeDtypeStruct((t, d), F32),
        compiler_params=_cparams(1),
        name=name,
    )(a, w, x, gate)


ROT_SHIFT = LANES - QK_ROPE


def _rope_tile(x, cos_t, sin_t):
    return x * cos_t + pltpu.roll(x, ROT_SHIFT, axis=1) * sin_t


def _q_proj_kernel(lat_ref, g_ref, w_ref, cos_ref, sin_ref, q_ref):
    cq = _rms(lat_ref[:, :Q_LORA], g_ref[...])
    q = jnp.dot(cq.astype(BF16), w_ref[...], preferred_element_type=F32)
    cos_t, sin_t = cos_ref[...], sin_ref[...]
    for h in range(MLA_HEADS):
        sl = slice(h * LANES, (h + 1) * LANES)
        q_ref[:, sl] = _rope_tile(q[:, sl], cos_t, sin_t).astype(q_ref.dtype)


def _kv_proj_kernel(lat_ref, g_ref, w_ref, cos_ref, sin_ref, k_ref, kv_ref):
    ckv = _rms(lat_ref[:, Q_LORA:Q_LORA + KV_LORA], g_ref[...])
    kv = jnp.dot(ckv.astype(BF16), w_ref[...], preferred_element_type=F32)
    kv_ref[...] = kv.astype(kv_ref.dtype)
    rope = _rope_tile(lat_ref[:, Q_LORA + KV_LORA:], cos_ref[...], sin_ref[...])
    lane = lax.broadcasted_iota(jnp.int32, rope.shape, 1)
    for h in range(MLA_HEADS):
        sl = slice(h * LANES, (h + 1) * LANES)
        k_ref[:, sl] = jnp.where(lane < QK_NOPE, kv[:, sl], rope).astype(k_ref.dtype)


def _mla_proj(kernel, lat, g, w, cos_t, sin_t, tm, n_out, name):
    t, nl = lat.shape
    n = w.shape[1]
    tab_blocks = cos_t.shape[0] // tm
    tab = lambda i: (i % tab_blocks, 0)
    return pl.pallas_call(
        kernel,
        grid=(t // tm,),
        in_specs=[pl.BlockSpec((tm, nl), lambda i: (i, 0)),
                  pl.BlockSpec((1, g.shape[-1]), lambda i: (0, 0)),
                  pl.BlockSpec(w.shape, lambda i: (0, 0)),
                  pl.BlockSpec((tm, LANES), tab),
                  pl.BlockSpec((tm, LANES), tab)],
        out_specs=[pl.BlockSpec((tm, n), lambda i: (i, 0))] * n_out,
        out_shape=[jax.ShapeDtypeStruct((t, n), BF16)] * n_out,
        compiler_params=_cparams(1),
        name=name,
    )(lat, g.reshape(1, -1), w, cos_t, sin_t)


def _attn_kernel(q_ref, kl_ref, vl_ref, kc_ref, vc_ref, o_ref):
    q = q_ref[...]
    nt = (((1,), (1,)), ((), ()))
    s1 = lax.dot_general(q, kl_ref[...], nt, preferred_element_type=F32)
    s2 = lax.dot_general(q, kc_ref[...], nt, preferred_element_type=F32)
    m = jnp.maximum(jnp.max(s1, axis=-1, keepdims=True), jnp.max(s2, axis=-1, keepdims=True))
    p1 = jnp.exp(s1 - m)
    p2 = jnp.exp(s2 - m)
    l = jnp.sum(p1, axis=-1, keepdims=True) + jnp.sum(p2, axis=-1, keepdims=True)
    o = jnp.dot(p1.astype(BF16), vl_ref[...], preferred_element_type=F32)
    o = o + jnp.dot(p2.astype(BF16), vc_ref[...], preferred_element_type=F32)
    o_ref[...] = (o / l).astype(o_ref.dtype)


def _attention(q, kl, kvl, kc, kvc, batch, tq):
    t = q.shape[0]
    s = t // batch
    c = kc.shape[0] // batch
    nq = s // tq
    kv_lat = pl.BlockSpec((s, LANES), lambda b, h, i: (b, h))
    kv_ctx = pl.BlockSpec((c, LANES), lambda b, h, i: (b, h))
    return pl.pallas_call(
        _attn_kernel,
        grid=(batch, MLA_HEADS, nq),
        in_specs=[pl.BlockSpec((tq, LANES), lambda b, h, i: (b * nq + i, h)), kv_lat, kv_lat, kv_ctx, kv_ctx],
        out_specs=pl.BlockSpec((tq, LANES), lambda b, h, i: (b * nq + i, h)),
        out_shape=jax.ShapeDtypeStruct(q.shape, BF16),
        compiler_params=_cparams(3),
        name="mla_attention",
    )(q, kl, kvl, kc, kvc)


def _short_conv_kernel(p_ref, w_ref, b_ref, o_ref):
    p = p_ref[...]
    n = p.shape[0]
    row = lax.broadcasted_iota(jnp.int32, p.shape, 0)
    prev = jnp.where(row == 0, 0.0, pltpu.roll(p, 1, axis=0))
    nxt = jnp.where(row == n - 1, 0.0, pltpu.roll(p, n - 1, axis=0))
    w = w_ref[...]
    o_ref[...] = b_ref[...] + w[0:1] * prev + w[1:2] * p + w[2:3] * nxt


def _short_conv(p, w, b, batch, tc):
    t, n = p.shape
    s = t // batch
    wpad = jnp.zeros((8, n), F32).at[:HY_SHORT].set(w)
    return pl.pallas_call(
        _short_conv_kernel,
        grid=(batch, n // tc),
        in_specs=[pl.BlockSpec((s, tc), lambda bi, j: (bi, j)),
                  pl.BlockSpec((8, tc), lambda bi, j: (0, j)),
                  pl.BlockSpec((1, tc), lambda bi, j: (0, j))],
        out_specs=pl.BlockSpec((s, tc), lambda bi, j: (bi, j)),
        out_shape=jax.ShapeDtypeStruct((t, n), F32),
        compiler_params=_cparams(2),
        name="hyena_short_conv",
    )(p, wpad, b.reshape(1, n))


def _dft_fwd_kernel(fc_ref, fs_ref, z_ref, kr_ref, ki_ref, y_ref, zb_ref):
    @pl.when(pl.program_id(2) == 0)
    def _():
        zb_ref[...] = z_ref[...].astype(BF16)

    zb = zb_ref[...]
    ur = jnp.dot(fc_ref[...], zb, preferred_element_type=F32)
    ui = jnp.dot(fs_ref[...], zb, preferred_element_type=F32)
    kr, ki = kr_ref[...], ki_ref[...]
    y_ref[0, 0] = (kr * ur - ki * ui).astype(y_ref.dtype)
    y_ref[0, 1] = (kr * ui + ki * ur).astype(y_ref.dtype)


def _dft_fwd(fc, fs, z, z_col0, kr, ki, batch, tf, tn):
    fp, s = fc.shape
    d = kr.shape[1]
    return pl.pallas_call(
        _dft_fwd_kernel,
        grid=(batch, d // tn, fp // tf),
        in_specs=[pl.BlockSpec((tf, s), lambda b, j, i: (i, 0)),
                  pl.BlockSpec((tf, s), lambda b, j, i: (i, 0)),
                  pl.BlockSpec((s, tn), lambda b, j, i: (b, z_col0 + j)),
                  pl.BlockSpec((tf, tn), lambda b, j, i: (i, j)),
                  pl.BlockSpec((tf, tn), lambda b, j, i: (i, j))],
        out_specs=pl.BlockSpec((1, 2, tf, tn), lambda b, j, i: (b, 0, i, j)),
        out_shape=jax.ShapeDtypeStruct((batch, 2, fp, d), BF16),
        scratch_shapes=[pltpu.VMEM((s, tn), BF16)],
        compiler_params=_cparams(3),
        name="hyena_dft_fwd",
    )(fc, fs, z, kr, ki)


def _dft_inv_kernel(g_ref, y_ref, z_ref, x_ref, b_ref, o_ref):
    conv = jnp.dot(g_ref[...], y_ref[...], preferred_element_type=F32)
    z = z_ref[...]
    o_ref[...] = x_ref[...] * (conv + b_ref[...] * z)


def _dft_inv(g, y, z, z_col0, xg, x_col0, bias, batch, tt, tn):
    s, k2 = g.shape
    d = bias.shape[1]
    nt = s // tt
    y2 = y.reshape(batch * k2, d)
    return pl.pallas_call(
        _dft_inv_kernel,
        grid=(batch, d // tn, nt),
        in_specs=[pl.BlockSpec((tt, k2), lambda b, j, i: (i, 0)),
                  pl.BlockSpec((k2, tn), lambda b, j, i: (b, j)),
                  pl.BlockSpec((tt, tn), lambda b, j, i: (b * nt + i, z_col0 + j)),
                  pl.BlockSpec((tt, tn), lambda b, j, i: (b * nt + i, x_col0 + j)),
                  pl.BlockSpec((1, tn), lambda b, j, i: (0, j))],
        out_specs=pl.BlockSpec((tt, tn), lambda b, j, i: (b * nt + i, j)),
        out_shape=jax.ShapeDtypeStruct((batch * s, d), F32),
        compiler_params=_cparams(3),
        name="hyena_dft_inv",
    )(g, y2, z, xg, bias)


def _hyena_time_filters(s, w1, b1, w2, b2, w3, b3, w4, sin_freq, deltas):
    t = jnp.linspace(0.0, 1.0, s, dtype=F32)[:, None]
    w = (2.0 * math.pi / s) * jnp.arange(s, dtype=F32)[:, None]
    f = jnp.linspace(1e-4, HY_BANDS - 1, HY_BANDS, dtype=F32)[None, :]
    z = jnp.concatenate([t, jnp.cos(f * w), -jnp.sin(f * w)], axis=-1)
    hp = lax.Precision.HIGHEST
    a = jnp.sin(sin_freq * (jnp.dot(z, w1, precision=hp) + b1))
    a = jnp.sin(sin_freq * (jnp.dot(a, w2, precision=hp) + b2))
    a = jnp.sin(sin_freq * (jnp.dot(a, w3, precision=hp) + b3))
    d = w4.shape[-1] // (2 * HY_ORDER)
    fpad = 128
    a_pad = jnp.zeros((s, fpad), F32).at[:, :a.shape[1]].set(a)
    w4_pad = jnp.zeros((fpad, w4.shape[1]), F32).at[:w4.shape[0]].set(w4)
    hi = a_pad.astype(BF16)
    lo = (a_pad - hi.astype(F32)).astype(BF16)
    w4_hi = w4_pad.astype(BF16)
    w4_lo = (w4_pad - w4_hi.astype(F32)).astype(BF16)
    tm = min(512, s)
    hf = (_mm(hi, w4_hi, tm, 512, fpad, name="hyena_filter_mm")
          + _mm(hi, w4_lo, tm, 512, fpad, name="hyena_filter_mm")
          + _mm(lo, w4_hi, tm, 512, fpad, name="hyena_filter_mm"))
    hf = hf.reshape(s, HY_ORDER, 2, d)
    decay = jnp.exp(-t[:, :, None] * jnp.abs(deltas.astype(F32)))
    hf = hf * decay[:, :, None, :]
    fwd, bwd = hf[:, :, 0], hf[:, :, 1]
    k = jnp.concatenate([fwd, jnp.zeros((1, HY_ORDER, d), F32), bwd[:0:-1]], axis=0)
    k = k / jnp.sum(jnp.abs(k), axis=0, keepdims=True)
    return k.reshape(2 * s, HY_ORDER * d)


def _dft_tables(s, fp):
    n = 2 * s
    f = jnp.arange(fp, dtype=jnp.int32)[:, None]
    valid = (f <= s)
    wgt = jnp.where((f == 0) | (f == s), 1.0, 2.0) / n

    def trig(nt):
        tt = jnp.arange(nt, dtype=jnp.int32)[None, :]
        ang = ((f * tt) % n).astype(F32) * (2.0 * math.pi / n)
        return jnp.where(valid, jnp.cos(ang), 0.0), jnp.where(valid, jnp.sin(ang), 0.0)

    c_s, s_s = trig(s)
    c_n, s_n = trig(n)
    fc, fs = c_s.astype(BF16), (-s_s).astype(BF16)
    fc_n, fs_n = c_n.astype(BF16), (-s_n).astype(BF16)
    g = jnp.concatenate([(c_s * wgt).T, (-s_s * wgt).T], axis=1).astype(BF16)
    return fc, fs, fc_n, fs_n, g


def _hyena_mixer(x, g_norm, shift, scale, gate, hy, batch):
    (w_in, conv_w, conv_b, w1, b1, w2, b2, w3, b3, w4, sin_freq, deltas, bias, w_out) = hy
    t, d = x.shape
    s = t // batch
    fp = -(-(s + 1) // LANES) * LANES
    tf = 384 if fp % 384 == 0 else LANES
    tn = min(512, d)
    tt = min(256, s)
    tm = min(512, s)

    p = _norm_mod_mm(x, g_norm, shift, scale, w_in.astype(BF16), tm, name="hyena_in_proj")
    acc = _short_conv(p, conv_w, conv_b, batch, min(256, d))

    fc, fs, fc_n, fs_n, g = _dft_tables(s, fp)
    k_time = _hyena_time_filters(s, w1, b1, w2, b2, w3, b3, w4, sin_freq, deltas)
    kb = k_time.astype(BF16)
    tkk = min(2048, 2 * s)
    kr = _mm(fc_n, kb, tf, tn, tkk, name="hyena_filter_dft")
    ki = _mm(fs_n, kb, tf, tn, tkk, name="hyena_filter_dft")

    nblk = d // tn
    z, z_col0 = acc, 0
    for n in range(HY_ORDER):
        krn, kin = kr[:, n * d:(n + 1) * d], ki[:, n * d:(n + 1) * d]
        y = _dft_fwd(fc, fs, z, z_col0, krn, kin, batch, tf, tn)
        z = _dft_inv(g, y, z, z_col0, acc, (n + 1) * nblk, bias[n:n + 1], batch, tt, tn)
        z_col0 = 0
    return _resid_mm(z, w_out.astype(BF16), x, gate, tm, name="hyena_out_proj")


def _argmax_rows(s, iota, n):
    m = jnp.max(s, axis=0, keepdims=True)
    idx = jnp.min(jnp.where(s == m, iota, n), axis=0, keepdims=True)
    return m, idx


def _peer_retrieve_kernel(q_ref, keys_ref, ids_ref, gates_ref, sv_ref, si_ref, eo_ref, go_ref):
    tb = q_ref.shape[0]
    nt = (((1,), (1,)), ((), ()))
    iota_k = lax.broadcasted_iota(jnp.int32, (PEER_NK, tb), 0)
    neg = jnp.float32(-jnp.inf)

    def half_body(hp, carry):
        q = q_ref[:, pl.ds(pl.multiple_of(hp * PEER_NK, PEER_NK), PEER_NK)].astype(BF16)
        s = lax.dot_general(keys_ref[hp], q, nt, preferred_element_type=F32)
        for r in range(PEER_TOPK):
            m, idx = _argmax_rows(s, iota_k, PEER_NK)
            sv_ref[hp, r:r + 1, :] = m
            si_ref[hp, r:r + 1, :] = idx
            s = jnp.where(iota_k == idx, neg, s)
        return carry

    lax.fori_loop(0, 2 * PEER_HEADS, half_body, 0)

    ncand = PEER_TOPK * PEER_TOPK
    iota_c = lax.broadcasted_iota(jnp.int32, (ncand, tb), 0)

    def head_body(h, carry):
        sv0, sv1 = sv_ref[2 * h], sv_ref[2 * h + 1]
        si0, si1 = si_ref[2 * h], si_ref[2 * h + 1]
        cand = jnp.concatenate([sv0[a:a + 1] + sv1 for a in range(PEER_TOPK)], axis=0)
        cidx = jnp.concatenate([si0[a:a + 1] * PEER_NK + si1 for a in range(PEER_TOPK)], axis=0)
        fs, es = [], []
        for r in range(PEER_TOPK):
            m, pos = _argmax_rows(cand, iota_c, ncand)
            hit = iota_c == pos
            fs.append(m)
            es.append(jnp.max(jnp.where(hit, cidx, 0), axis=0, keepdims=True))
            cand = jnp.where(hit, neg, cand)
        fs = jnp.concatenate(fs, axis=0)
        ex = jnp.exp(fs - fs[0:1])
        gates = ex / jnp.sum(ex, axis=0, keepdims=True)
        row0 = pl.multiple_of(h * PEER_TOPK, PEER_TOPK)
        eo_ref[pl.ds(row0, PEER_TOPK), :] = jnp.concatenate(es, axis=0)
        go_ref[pl.ds(row0, PEER_TOPK), :] = gates
        return carry

    lax.fori_loop(0, PEER_HEADS, head_body, 0)
    ids_ref[...] = jnp.transpose(eo_ref[...])
    gates_ref[...] = jnp.transpose(go_ref[...])


def _peer_retrieve(q, keys, tb):
    t = q.shape[0]
    nh2 = 2 * PEER_HEADS
    return pl.pallas_call(
        _peer_retrieve_kernel,
        grid=(t // tb,),
        in_specs=[pl.BlockSpec((tb, q.shape[1]), lambda i: (i, 0)),
                  pl.BlockSpec(keys.shape, lambda i: (0, 0, 0))],
        out_specs=[pl.BlockSpec((tb, PEER_PAIRS), lambda i: (i, 0))] * 2,
        out_shape=[jax.ShapeDtypeStruct((t, PEER_PAIRS), jnp.int32), jax.ShapeDtypeStruct((t, PEER_PAIRS), F32)],
        scratch_shapes=[pltpu.VMEM((nh2, PEER_TOPK, tb), F32), pltpu.VMEM((nh2, PEER_TOPK, tb), jnp.int32),
                        pltpu.VMEM((PEER_PAIRS, tb), jnp.int32), pltpu.VMEM((PEER_PAIRS, tb), F32)],
        compiler_params=_cparams(1),
        name="peer_retrieve",
    )(q, keys)


PEER_TB = 8


def _peer_expert_kernel(ids_cur, ids_nxt, gates_ref, h_ref, x_ref, gt_ref, u_hbm, v_hbm, o_ref,
                        ubuf, vbuf, sem, *, nblk):
    i = pl.program_id(0)
    tb = h_ref.shape[0]
    rows = tb * PEER_PAIRS
    slot = i % 2

    def row_copies(ids_ref, r, dst_slot):
        e = ids_ref[lax.shift_right_logical(r, PAIR_SHIFT), jnp.bitwise_and(r, PEER_PAIRS - 1)]
        cu = pltpu.make_async_copy(u_hbm.at[pl.ds(e, 1), :], ubuf.at[dst_slot, pl.ds(r, 1), :], sem.at[0, dst_slot])
        cv = pltpu.make_async_copy(v_hbm.at[pl.ds(e, 1), :], vbuf.at[dst_slot, pl.ds(r, 1), :], sem.at[1, dst_slot])
        return cu, cv

    def issue(ids_ref, dst_slot):
        def body(r, carry):
            cu, cv = row_copies(ids_ref, r, dst_slot)
            cu.start()
            cv.start()
            return carry

        lax.fori_loop(0, rows, body, 0, unroll=8)

    @pl.when(i == 0)
    def _():
        issue(ids_cur, 0)

    @pl.when(i + 1 < nblk)
    def _():
        issue(ids_nxt, 1 - slot)

    pltpu.make_async_copy(u_hbm.at[pl.ds(0, rows), :], ubuf.at[slot], sem.at[0, slot]).wait()
    pltpu.make_async_copy(v_hbm.at[pl.ds(0, rows), :], vbuf.at[slot], sem.at[1, slot]).wait()

    g = gates_ref[...]
    g_t = jnp.transpose(jnp.concatenate([g, jnp.zeros((PEER_PAIRS - tb, PEER_PAIRS), F32)], axis=0))
    outs = []
    for j in range(tb):
        u = ubuf[slot, pl.ds(j * PEER_PAIRS, PEER_PAIRS), :]
        v = vbuf[slot, pl.ds(j * PEER_PAIRS, PEER_PAIRS), :]
        sc = jnp.sum(u * h_ref[j:j + 1, :], axis=-1, keepdims=True)
        act = 0.5 * sc * (1.0 + lax.erf(sc * (1.0 / math.sqrt(2.0))))
        wgt = act * g_t[:, j:j + 1]
        outs.append(jnp.sum(wgt * v, axis=0, keepdims=True))
    o_ref[...] = x_ref[...] + gt_ref[0] * jnp.concatenate(outs, axis=0)


def _peer_experts(ids, gates, h, x, gate, u, v):
    t, d = x.shape
    tb = PEER_TB
    nblk = t // tb
    nb = gate.shape[0]
    blocks_per_batch = t // nb // tb
    rows = tb * PEER_PAIRS
    smem = functools.partial(pl.BlockSpec, memory_space=pltpu.SMEM)
    return pl.pallas_call(
        functools.partial(_peer_expert_kernel, nblk=nblk),
        grid=(nblk,),
        in_specs=[smem((tb, PEER_PAIRS), lambda i: (i, 0)),
                  smem((tb, PEER_PAIRS), lambda i: (jnp.minimum(i + 1, nblk - 1), 0)),
                  pl.BlockSpec((tb, PEER_PAIRS), lambda i: (i, 0)),
                  pl.BlockSpec((tb, d), lambda i: (i, 0)),
                  pl.BlockSpec((tb, d), lambda i: (i, 0)),
                  pl.BlockSpec((1, 1, d), lambda i: (i // blocks_per_batch, 0, 0)),
                  pl.BlockSpec(memory_space=pl.ANY),
                  pl.BlockSpec(memory_space=pl.ANY)],
        out_specs=pl.BlockSpec((tb, d), lambda i: (i, 0)),
        out_shape=jax.ShapeDtypeStruct((t, d), F32),
        scratch_shapes=[pltpu.VMEM((2, rows, d), F32), pltpu.VMEM((2, rows, d), F32),
                        pltpu.SemaphoreType.DMA((2, 2))],
        compiler_params=_cparams(1),
        name="peer_experts",
    )(ids, ids, gates, h, x, gate, u, v)


def _peer_layer(x, g_norm, shift, scale, gate, wq, keys, u, v, batch):
    t, d = x.shape
    tm = min(512, t // batch)
    q, h = _norm_mod_mm(x, g_norm, shift, scale, wq.astype(BF16), tm, with_h=True, name="peer_query")
    keys2 = keys.reshape(2 * PEER_HEADS, PEER_NK, keys.shape[-1]).astype(BF16)
    ids, gates = _peer_retrieve(q, keys2, min(256, t))
    return _peer_experts(ids, gates, h, x, gate, u, v)


def _final_norm_kernel(x_ref, g_ref, o_ref):
    o_ref[...] = _rms(x_ref[...], g_ref[...])


def _final_norm(x, g, tm):
    t, d = x.shape
    return pl.pallas_call(
        _final_norm_kernel,
        grid=(t // tm,),
        in_specs=[pl.BlockSpec((tm, d), lambda i: (i, 0)), pl.BlockSpec((1, d), lambda i: (0, 0))],
        out_specs=pl.BlockSpec((tm, d), lambda i: (i, 0)),
        out_shape=jax.ShapeDtypeStruct((t, d), F32),
        compiler_params=_cparams(1),
        name="final_norm",
    )(x, g.reshape(1, d))


def _rope_rotation_matrix():
    r = [[0.0] * QK_ROPE for _ in range(QK_ROPE)]
    q = QK_ROPE // 4
    for base in (0, 2 * q):
        for dd in range(q):
            r[base + q + dd][base + dd] = -1.0
            r[base + dd][base + q + dd] = 1.0
    return jnp.array(r, F32)


def _rope_tables(s, scale):
    tpos = jnp.arange(s, dtype=jnp.int32)
    row = (tpos // GRID_W).astype(F32)
    col = (tpos % GRID_W).astype(F32)
    n_freq = QK_ROPE // 4
    inv = ROPE_BASE ** (-jnp.arange(n_freq, dtype=F32) / n_freq)
    ar = row[:, None] * inv
    ac = col[:, None] * inv
    ang = jnp.concatenate([ar, ar, ac, ac], axis=-1)
    ones = jnp.ones((s, QK_NOPE), F32)
    zeros_n = jnp.zeros((s, QK_NOPE), F32)
    zeros_r = jnp.zeros((s, QK_ROPE), F32)
    cos_t = jnp.concatenate([ones, jnp.cos(ang), zeros_r], axis=1) * scale
    sin_t = jnp.concatenate([zeros_n, jnp.sin(ang), zeros_r], axis=1) * scale
    return cos_t, sin_t


def _mla_mixer(x, ctx, g_norm, m_lat, m_ctx, w_dkv, g_q, w_uq, g_kv, w_ukv, w_o, batch):
    t, d = x.shape
    s = t // batch
    c = ctx.shape[0] // batch
    rot = _rope_rotation_matrix()
    w_rope = w_dkv[:, Q_LORA + KV_LORA:]
    w_dkv_ext = jnp.concatenate([w_dkv[:, :Q_LORA + KV_LORA], jnp.zeros((d, QK_NOPE), F32), w_rope, w_rope @ rot],
                                axis=1).astype(BF16)
    wq3 = w_uq.reshape(Q_LORA, MLA_HEADS, QK_NOPE + QK_ROPE)
    wq_ext = jnp.concatenate([wq3, jnp.einsum("chr,rs->chs", wq3[:, :, QK_NOPE:], rot)], axis=2)
    wq_ext = wq_ext.reshape(Q_LORA, MLA_HEADS * LANES).astype(BF16)
    wo3 = w_o.reshape(MLA_HEADS, V_DIM, d)
    wo_ext = jnp.concatenate([jnp.zeros((MLA_HEADS, LANES - V_DIM, d), F32), wo3], axis=1)
    wo_ext = wo_ext.reshape(MLA_HEADS * LANES, d).astype(BF16)
    w_ukv_b = w_ukv.astype(BF16)

    tm = min(512, s)
    tmc = min(512, c)
    lat_l = _norm_mod_mm(x, g_norm, m_lat[0], m_lat[1], w_dkv_ext, tm, name="mla_down_proj")
    lat_c = _norm_mod_mm(ctx, g_norm, m_ctx[0], m_ctx[1], w_dkv_ext, tmc, name="mla_down_proj_ctx")

    cos_k, sin_k = _rope_tables(s, 1.0)
    cos_q, sin_q = _rope_tables(s, float((QK_NOPE + QK_ROPE) ** -0.5))
    lane = jnp.arange(LANES)[None, :]
    cos_c = jnp.broadcast_to(jnp.where(lane < QK_NOPE + QK_ROPE, 1.0, 0.0).astype(F32), (c, LANES))
    sin_c = jnp.zeros((c, LANES), F32)

    (q,) = _mla_proj(_q_proj_kernel, lat_l, g_q, wq_ext, cos_q, sin_q, tm, 1, "mla_q_proj")
    kl, kvl = _mla_proj(_kv_proj_kernel, lat_l, g_kv, w_ukv_b, cos_k, sin_k, tm, 2, "mla_kv_proj")
    kc, kvc = _mla_proj(_kv_proj_kernel, lat_c, g_kv, w_ukv_b, cos_c, sin_c, tmc, 2, "mla_kv_proj_ctx")
    o = _attention(q, kl, kvl, kc, kvc, batch, min(512, s))
    return _resid_mm(o, wo_ext, x, m_lat[2], tm, name="mla_out_proj")


def kernel(x, c, ctx, c_ctx, mod_w, mod_b, norm_mix_g, norm_ffn_g, final_g, mla_w_dkv, mla_q_norm_g, mla_w_uq, mla_kv_norm_g, mla_w_ukv, mla_w_o, hy_w_in, hy_conv_w, hy_conv_b, hy_w1, hy_b1, hy_w2, hy_b2, hy_w3, hy_b3, hy_w4, hy_sin_freq, hy_deltas, hy_bias, hy_w_out, peer_wq, peer_keys, peer_u, peer_v):
    batch, s, d = x.shape
    depth = mod_w.shape[0]
    xt = x.reshape(batch * s, d)
    ctx_t = ctx.reshape(-1, d)

    rows = -(-(batch + 1) // 8) * 8
    sc_all = jnp.zeros((rows, d), F32).at[:batch].set(jax.nn.silu(c)).at[batch].set(jax.nn.silu(c_ctx))

    for i in range(depth):
        j = i // 2
        m = _mm_bias(sc_all, mod_w[i], mod_b[i], 6 * d // 4)
        m_lat = [m[:batch, k * d:(k + 1) * d].reshape(batch, 1, d) for k in range(6)]
        m_ctx = [m[batch:batch + 1, k * d:(k + 1) * d].reshape(1, 1, d) for k in range(6)]
        if i % 2 == 0:
            xt = _mla_mixer(xt, ctx_t, norm_mix_g[i], m_lat, m_ctx, mla_w_dkv[j], mla_q_norm_g[j], mla_w_uq[j],
                            mla_kv_norm_g[j], mla_w_ukv[j], mla_w_o[j], batch)
        else:
            hy = (hy_w_in[j], hy_conv_w[j], hy_conv_b[j], hy_w1[j], hy_b1[j], hy_w2[j], hy_b2[j], hy_w3[j], hy_b3[j],
                  hy_w4[j], hy_sin_freq[j], hy_deltas[j], hy_bias[j], hy_w_out[j])
            xt = _hyena_mixer(xt, norm_mix_g[i], m_lat[0], m_lat[1], m_lat[2], hy, batch)
        xt = _peer_layer(xt, norm_ffn_g[i], m_lat[3], m_lat[4], m_lat[5], peer_wq[i], peer_keys[i], peer_u[i],
                         peer_v[i], batch)
    return _final_norm(xt, final_g, min(512, s)).reshape(batch, s, d)
```
